```python
import math
import jax
import jax.numpy as jnp
from jax import lax
import numpy as np

D_MODEL = 2048
BATCH = 2
SEQ = 16384
DEPTH = 1
DEC_BATCH = 4
DEC_SEQ = 8192
PAST_LEN = 128

N_MEM = 256
CHUNK = 128
A_GROUPS = 8
A_WIDTH = D_MODEL
B_HEADS = 8
B_HEAD_DIM = 128
B_QK_WIDTH = B_HEADS * 2 * B_HEAD_DIM
B_V_WIDTH = B_HEADS * 2 * B_HEAD_DIM
Q_BLOCK = 128
N_BUCKETS = 32
MAX_DISTANCE = 128
C_HEADS = 4
C_HEAD_DIM = 128
C_WIDTH = C_HEADS * C_HEAD_DIM
D_FF = -(-8 * D_MODEL // (3 * 256)) * 256
N_IN = 2 * A_WIDTH + 2 * B_QK_WIDTH + B_V_WIDTH + 2 * D_MODEL
EPS = 1e-6

kernel_name = 'hybrid_gmlp_diffattn_encoder'


def _rmsnorm(x, g, eps=EPS):
    xf = x.astype(jnp.float32)
    y = xf * lax.rsqrt(jnp.mean(xf * xf, axis=-1, keepdims=True) + eps)
    return (y * g.astype(jnp.float32)).astype(x.dtype)


def _layernorm(x, g, b):
    xf = x.astype(jnp.float32)
    xc = xf - jnp.mean(xf, axis=-1, keepdims=True)
    y = xc * lax.rsqrt(jnp.mean(xc * xc, axis=-1, keepdims=True) + EPS)
    return (y * g.astype(jnp.float32) + b.astype(jnp.float32)).astype(x.dtype)


def _t5_bucket(rel):
    half = N_BUCKETS // 2
    max_exact = half // 2
    n = jnp.abs(rel)
    nf = jnp.maximum(n, 1).astype(jnp.float32)
    large = max_exact + (jnp.log(nf / max_exact) / math.log(MAX_DISTANCE / max_exact)
                         * (half - max_exact)).astype(jnp.int32)
    large = jnp.minimum(large, half - 1)
    return jnp.where(rel > 0, half, 0) + jnp.where(n < max_exact, n, large)


def _chunked_spatial_gating(u, v, ln_g, ln_b, w_s, b_s):
    bsz, seq, _ = v.shape
    u = jax.nn.gelu(u)
    v = _layernorm(jax.nn.gelu(v), ln_g, ln_b)
    vc = v.reshape(bsz, seq // CHUNK, CHUNK, A_GROUPS, A_WIDTH // A_GROUPS)
    mixed = jnp.einsum('gij,bcjgd->bcigd', w_s, vc) + jnp.transpose(b_s)[:, :, None]
    return u * mixed.reshape(bsz, seq, A_WIDTH)


def _diff_attention(q, k, v, rel_table, lam, lam_init, subln_g):
    bsz, seq, _ = q.shape
    d = B_HEAD_DIM
    n_blk = seq // Q_BLOCK
    q = q.reshape(bsz, n_blk, Q_BLOCK, B_HEADS, 2, d).transpose(4, 1, 0, 3, 2, 5)
    k = k.reshape(bsz, seq, B_HEADS, 2, d).transpose(3, 0, 2, 1, 4)
    v = v.reshape(bsz, seq, B_HEADS, 2 * d).transpose(0, 2, 1, 3)
    scale = d ** -0.5
    k_pos = jnp.arange(seq, dtype=jnp.int32)

    def one_block(args):
        q1b, q2b, start = args
        q_pos = start + jnp.arange(Q_BLOCK, dtype=jnp.int32)
        bucket = _t5_bucket(k_pos[None, :] - q_pos[:, None])
        bias = jnp.transpose(rel_table[bucket].astype(jnp.float32), (2, 0, 1))
        p1 = jax.nn.softmax(jnp.einsum('bhqd,bhkd->bhqk', q1b, k[0]).astype(jnp.float32) * scale + bias, axis=-1)
        p2 = jax.nn.softmax(jnp.einsum('bhqd,bhkd->bhqk', q2b, k[1]).astype(jnp.float32) * scale + bias, axis=-1)
        a = (p1 - lam * p2).astype(v.dtype)
        return jnp.einsum('bhqk,bhke->bhqe', a, v)

    starts = jnp.arange(n_blk, dtype=jnp.int32) * Q_BLOCK
    o = lax.map(one_block, (q[0], q[1], starts))
    o = _rmsnorm(o, subln_g, eps=1e-5) * (1.0 - lam_init)
    return o.transpose(1, 0, 3, 2, 4).reshape(bsz, seq, B_HEADS * 2 * d)


def _memory_cross_attention(x_n, mem_n, w_cq, w_ck, w_cv, w_co):
    bsz, seq, _ = x_n.shape
    q = (x_n @ w_cq).reshape(bsz, seq, C_HEADS, C_HEAD_DIM)
    k = (mem_n @ w_ck).reshape(bsz, N_MEM, C_HEADS, C_HEAD_DIM)
    v = (mem_n @ w_cv).reshape(bsz, N_MEM, C_HEADS, C_HEAD_DIM)
    s = jnp.einsum('bqhd,bkhd->bhqk', q, k).astype(jnp.float32) * (C_HEAD_DIM ** -0.5)
    p = jax.nn.softmax(s, axis=-1).astype(v.dtype)
    o = jnp.einsum('bhqk,bkhd->bqhd', p, v).reshape(bsz, seq, C_WIDTH)
    return o @ w_co


def _encoder_trunk(x, mem, rel_bias_table, norm_mix_g, w_in, ln_v_g, ln_v_b, w_spatial, b_spatial,
                   lambda_q1, lambda_k1, lambda_q2, lambda_k2, subln_g, w_proj_a, w_proj_b, w_out,
                   norm_cross_g, norm_mem_g, w_cq, w_ck, w_cv, w_co, norm_ffn_g, w_ffn_in, w_ffn_out,
                   norm_final_g):
    splits = [A_WIDTH, 2 * A_WIDTH, 2 * A_WIDTH + B_QK_WIDTH, 2 * A_WIDTH + 2 * B_QK_WIDTH,
              2 * A_WIDTH + 2 * B_QK_WIDTH + B_V_WIDTH,
              2 * A_WIDTH + 2 * B_QK_WIDTH + B_V_WIDTH + D_MODEL]
    for l in range(DEPTH):
        lam_init = 0.8 - 0.6 * math.exp(-0.3 * l)
        h = _rmsnorm(x, norm_mix_g[l])
        u, va, qb, kb, vb, g_a, g_b = jnp.split(h @ w_in[l], splits, axis=-1)
        o_a = _chunked_spatial_gating(u, va, ln_v_g[l], ln_v_b[l], w_spatial[l], b_spatial[l])
        lam = (jnp.exp(jnp.sum(lambda_q1[l].astype(jnp.float32) * lambda_k1[l].astype(jnp.float32)))
               - jnp.exp(jnp.sum(lambda_q2[l].astype(jnp.float32) * lambda_k2[l].astype(jnp.float32)))
               + lam_init)
        o_b = _diff_attention(qb, kb, vb, rel_bias_table, lam, lam_init, subln_g[l])
        merged = jax.nn.sigmoid(g_a) * (o_a @ w_proj_a[l]) + jax.nn.sigmoid(g_b) * (o_b @ w_proj_b[l])
        x = x + merged @ w_out[l]
        x = x + _memory_cross_attention(_rmsnorm(x, norm_cross_g[l]), _rmsnorm(mem, norm_mem_g[l]),
                                        w_cq[l], w_ck[l], w_cv[l], w_co[l])
        gate, up = jnp.split(_rmsnorm(x, norm_ffn_g[l]) @ w_ffn_in[l], 2, axis=-1)
        x = x + (jax.nn.silu(gate) * up) @ w_ffn_out[l]
    return _rmsnorm(x, norm_final_g)


def setup_inputs(seed: int = 0) -> dict:
    key = jax.random.key(seed)
    ks = jax.random.split(key, 32)

    def nrm(k, shape, scale):
        return jax.random.normal(k, shape, dtype=jnp.float32) * scale

    def gain(k, shape):
        return 1.0 + nrm(k, shape, 0.02)

    L = DEPTH
    return {
        'x_prompt': nrm(ks[0], (BATCH, SEQ, D_MODEL), 1.0),
        'x_sample': nrm(ks[1], (DEC_BATCH, DEC_SEQ, D_MODEL), 1.0),
        'mem_prompt': nrm(ks[2], (BATCH, N_MEM, D_MODEL), 1.0),
        'mem_sample': nrm(ks[3], (DEC_BATCH, N_MEM, D_MODEL), 1.0),
        'rel_bias_table': nrm(ks[4], (N_BUCKETS, B_HEADS), 0.5),
        'norm_mix_g': gain(ks[5], (L, D_MODEL)),
        'w_in': nrm(ks[6], (L, D_MODEL, N_IN), D_MODEL ** -0.5),
        'ln_v_g': gain(ks[7], (L, A_WIDTH)),
        'ln_v_b': nrm(ks[8], (L, A_WIDTH), 0.02),
        'w_spatial': nrm(ks[9], (L, A_GROUPS, CHUNK, CHUNK), CHUNK ** -0.5),
        'b_spatial': nrm(ks[10], (L, A_GROUPS, CHUNK), 0.02),
        'lambda_q1': nrm(ks[11], (L, B_HEAD_DIM), 0.1),
        'lambda_k1': nrm(ks[12], (L, B_HEAD_DIM), 0.1),
        'lambda_q2': nrm(ks[13], (L, B_HEAD_DIM), 0.1),
        'lambda_k2': nrm(ks[14], (L, B_HEAD_DIM), 0.1),
        'subln_g': gain(ks[15], (L, 2 * B_HEAD_DIM)),
        'w_proj_a': nrm(ks[16], (L, A_WIDTH, D_MODEL), A_WIDTH ** -0.5),
        'w_proj_b': nrm(ks[17], (L, B_V_WIDTH, D_MODEL), B_V_WIDTH ** -0.5),
        'w_out': nrm(ks[18], (L, D_MODEL, D_MODEL), D_MODEL ** -0.5),
        'norm_cross_g': gain(ks[19], (L, D_MODEL)),
        'norm_mem_g': gain(ks[20], (L, D_MODEL)),
        'w_cq': nrm(ks[21], (L, D_MODEL, C_WIDTH), D_MODEL ** -0.5),
        'w_ck': nrm(ks[22], (L, D_MODEL, C_WIDTH), D_MODEL ** -0.5),
        'w_cv': nrm(ks[23], (L, D_MODEL, C_WIDTH), D_MODEL ** -0.5),
        'w_co': nrm(ks[24], (L, C_WIDTH, D_MODEL), C_WIDTH ** -0.5),
        'norm_ffn_g': gain(ks[25], (L, D_MODEL)),
        'w_ffn_in': nrm(ks[26], (L, D_MODEL, 2 * D_FF), D_MODEL ** -0.5),
        'w_ffn_out': nrm(ks[27], (L, D_FF, D_MODEL), D_FF ** -0.5),
        'norm_final_g': gain(ks[28], (D_MODEL,)),
    }


def reference(x_prompt, x_sample, mem_prompt, mem_sample, rel_bias_table, norm_mix_g, w_in, ln_v_g, ln_v_b,
              w_spatial, b_spatial, lambda_q1, lambda_k1, lambda_q2, lambda_k2, subln_g, w_proj_a, w_proj_b,
              w_out, norm_cross_g, norm_mem_g, w_cq, w_ck, w_cv, w_co, norm_ffn_g, w_ffn_in, w_ffn_out,
              norm_final_g):
    weights = (rel_bias_table, norm_mix_g, w_in, ln_v_g, ln_v_b, w_spatial, b_spatial,
               lambda_q1, lambda_k1, lambda_q2, lambda_k2, subln_g, w_proj_a, w_proj_b, w_out,
               norm_cross_g, norm_mem_g, w_cq, w_ck, w_cv, w_co, norm_ffn_g, w_ffn_in, w_ffn_out,
               norm_final_g)
    y_prompt = _encoder_trunk(x_prompt, mem_prompt, *weights)
    y_sample = _encoder_trunk(x_sample, mem_sample, *weights)
    return (y_prompt, y_sample)
```

```python
import functools
import math

import jax
import jax.numpy as jnp
from jax import lax
from jax.experimental import pallas as pl
from jax.experimental.pallas import tpu as pltpu

F32 = jnp.float32
BF16 = jnp.bfloat16

D_MODEL = 2048
N_MEM = 256
CHUNK = 128
A_GROUPS = 8
A_GROUP_W = D_MODEL // A_GROUPS
B_HEADS = 8
B_HEAD_DIM = 128
B_HEAD_W = 2 * B_HEAD_DIM
N_BUCKETS = 32
MAX_DISTANCE = 128
C_HEADS = 4
C_HEAD_DIM = 128
C_WIDTH = C_HEADS * C_HEAD_DIM
D_FF = -(-8 * D_MODEL // (3 * 256)) * 256
N_IN = 7 * D_MODEL
EPS = 1e-6
SUBLN_EPS = 1e-5
LOG2E = 1.4426950408889634

COL_U, COL_VA, COL_Q, COL_K, COL_V, COL_GA, COL_GB = range(7)
HEADS_PER_COL = D_MODEL // B_HEAD_W

VMEM_LIMIT = 56 * 1024 * 1024
IN_TM, IN_TN = 1024, 1024
SG_TM = 512
ATTN_BLK = 512
MERGE_TM = 256
CROSS_TM = 512
FFN_TM, FFN_FC = 512, 512


def _params(*sem):
    return pltpu.CompilerParams(dimension_semantics=sem, vmem_limit_bytes=VMEM_LIMIT)


def _rms(xf, g, eps):
    return xf * lax.rsqrt(jnp.mean(xf * xf, axis=-1, keepdims=True) + eps) * g


def _dot(a, b):
    return jnp.dot(a, b, preferred_element_type=F32)


def _dot_nt(a, b):
    return lax.dot_general(a, b, (((1,), (1,)), ((), ())), preferred_element_type=F32)


def _bias_kernel(table_ref, tiles_ref, far_ref, *, blk):
    h = pl.program_id(0)
    half = N_BUCKETS // 2
    max_exact = half // 2

    def bias_of(rel):
        n = jnp.abs(rel)
        nf = jnp.maximum(n, 1).astype(F32)
        large = max_exact + (jnp.log(nf / max_exact) / math.log(MAX_DISTANCE / max_exact)
                             * (half - max_exact)).astype(jnp.int32)
        large = jnp.minimum(large, half - 1)
        bucket = jnp.where(rel > 0, half, 0) + jnp.where(n < max_exact, n, large)
        val = jnp.zeros(rel.shape, F32)
        for b in range(N_BUCKETS):
            val = jnp.where(bucket == b, table_ref[b, h], val)
        return val * LOG2E

    row = lax.broadcasted_iota(jnp.int32, (blk, blk), 0)
    col = lax.broadcasted_iota(jnp.int32, (blk, blk), 1)
    for d in (-1, 0, 1):
        tiles_ref[d + 1] = bias_of(d * blk + col - row)
    far = jnp.full((8, 128), MAX_DISTANCE, jnp.int32)
    far_ref[0] = bias_of(-far)
    far_ref[1] = bias_of(far)


def _bias_tiles(table, blk):
    return pl.pallas_call(
        functools.partial(_bias_kernel, blk=blk),
        grid=(B_HEADS,),
        in_specs=[pl.BlockSpec(memory_space=pltpu.SMEM)],
        out_specs=[pl.BlockSpec((None, 3, blk, blk), lambda h: (h, 0, 0, 0)),
                   pl.BlockSpec((None, 2, 8, 128), lambda h: (h, 0, 0, 0))],
        out_shape=[jax.ShapeDtypeStruct((B_HEADS, 3, blk, blk), F32),
                   jax.ShapeDtypeStruct((B_HEADS, 2, 8, 128), F32)],
        compiler_params=_params("arbitrary"),
        name="bias_tiles",
    )(table)


def _in_proj_kernel(x_ref, g_ref, w_ref, cs_ref, o_ref, h_scr):
    @pl.when(pl.program_id(1) == 0)
    def _():
        h_scr[...] = _rms(x_ref[...], g_ref[...], EPS).astype(BF16)

    o_ref[...] = (_dot(h_scr[...], w_ref[...]) * cs_ref[...]).astype(o_ref.dtype)


def _in_proj(x2d, g, w, colscale):
    t = x2d.shape[0]
    tm = min(IN_TM, t)
    return pl.pallas_call(
        _in_proj_kernel,
        grid=(t // tm, N_IN // IN_TN),
        in_specs=[pl.BlockSpec((tm, D_MODEL), lambda i, j: (i, 0)),
                  pl.BlockSpec((1, D_MODEL), lambda i, j: (0, 0)),
                  pl.BlockSpec((D_MODEL, IN_TN), lambda i, j: (0, j)),
                  pl.BlockSpec((1, IN_TN), lambda i, j: (0, j))],
        out_specs=pl.BlockSpec((tm, IN_TN), lambda i, j: (i, j)),
        out_shape=jax.ShapeDtypeStruct((t, N_IN), BF16),
        scratch_shapes=[pltpu.VMEM((tm, D_MODEL), BF16)],
        compiler_params=_params("parallel", "arbitrary"),
        name="in_proj",
    )(x2d, g, w, colscale)


def _spatial_kernel(u_ref, v_ref, lng_ref, lnb_ref, ws_ref, bs_ref, o_ref, *, n_chunks):
    for c in range(n_chunks):
        rows = slice(c * CHUNK, (c + 1) * CHUNK)
        gu = jax.nn.gelu(u_ref[rows, :].astype(F32))
        gv = jax.nn.gelu(v_ref[rows, :].astype(F32))
        vc = gv - jnp.mean(gv, axis=-1, keepdims=True)
        vn = vc * lax.rsqrt(jnp.mean(vc * vc, axis=-1, keepdims=True) + EPS)
        vn = (vn * lng_ref[...] + lnb_ref[...]).astype(BF16)
        for g in range(A_GROUPS):
            cols = slice(g * A_GROUP_W, (g + 1) * A_GROUP_W)
            mixed = _dot(ws_ref[g], vn[:, cols]) + bs_ref[:, cols]
            o_ref[rows, cols] = (gu[:, cols] * mixed).astype(o_ref.dtype)


def _spatial(proj, ln_g, ln_b, w_s, b_full):
    t = proj.shape[0]
    tm = min(SG_TM, t)
    return pl.pallas_call(
        functools.partial(_spatial_kernel, n_chunks=tm // CHUNK),
        grid=(t // tm,),
        in_specs=[pl.BlockSpec((tm, D_MODEL), lambda i: (i, COL_U)),
                  pl.BlockSpec((tm, D_MODEL), lambda i: (i, COL_VA)),
                  pl.BlockSpec((1, D_MODEL), lambda i: (0, 0)),
                  pl.BlockSpec((1, D_MODEL), lambda i: (0, 0)),
                  pl.BlockSpec((A_GROUPS, CHUNK, CHUNK), lambda i: (0, 0, 0)),
                  pl.BlockSpec((CHUNK, D_MODEL), lambda i: (0, 0))],
        out_specs=pl.BlockSpec((tm, D_MODEL), lambda i: (i, 0)),
        out_shape=jax.ShapeDtypeStruct((t, D_MODEL), BF16),
        compiler_params=_params("parallel"),
        name="spatial_gating",
    )(proj, proj, ln_g, ln_b, w_s, b_full)


def _attn_kernel(q_ref, k_ref, v_ref, bias_ref, far_ref, lq1_ref, lk1_ref, lq2_ref, lk2_ref, sg_ref,
                 o_ref, m_scr, l_scr, acc_scr, *, blk, n_blk, lam_init):
    i = pl.program_id(2)
    m_scr[...] = jnp.full(m_scr.shape, -jnp.inf, F32)
    l_scr[...] = jnp.zeros(l_scr.shape, F32)
    acc_scr[...] = jnp.zeros(acc_scr.shape, F32)
    lane_reps = blk // 128

    def tile(j, mode):
        start = pl.multiple_of(j * blk, blk)
        kc = k_ref[pl.ds(start, blk), :]
        vc = v_ref[pl.ds(start, blk), :]
        for mp in range(2):
            lanes = slice(mp * B_HEAD_DIM, (mp + 1) * B_HEAD_DIM)
            s = _dot_nt(q_ref[:, lanes], kc[:, lanes])
            m_prev = m_scr[mp]
            if mode == "left" or mode == "right":
                c = jnp.broadcast_to(far_ref[0 if mode == "left" else 1, 0:1, :], m_prev.shape)
                m_next = jnp.maximum(m_prev, jnp.max(s, axis=1, keepdims=True) + c)
                shift = m_next - c
            else:
                s = s + bias_ref[mode + 1]
                m_next = jnp.maximum(m_prev, jnp.max(s, axis=1, keepdims=True))
                shift = m_next
            p = jnp.exp2(s - jnp.concatenate([shift] * lane_reps, axis=1))
            alpha = jnp.exp2(m_prev - m_next)
            l_scr[mp] = alpha * l_scr[mp] + jnp.sum(p, axis=1, keepdims=True)
            m_scr[mp] = m_next
            pv = _dot(p.astype(BF16), vc)
            acc_scr[mp] = acc_scr[mp] * jnp.concatenate([alpha, alpha], axis=1) + pv

    def left_body(j, carry):
        tile(j, "left")
        return carry

    def right_body(j, carry):
        tile(j, "right")
        return carry

    lax.fori_loop(0, jnp.maximum(i - 1, 0), left_body, 0)
    for d in (-1, 0, 1):
        @pl.when(jnp.logical_and(i + d >= 0, i + d < n_blk))
        def _(d=d):
            tile(i + d, d)
    lax.fori_loop(i + 2, n_blk, right_body, 0)

    lam = (jnp.exp(jnp.sum(lq1_ref[...] * lk1_ref[...], axis=-1, keepdims=True))
           - jnp.exp(jnp.sum(lq2_ref[...] * lk2_ref[...], axis=-1, keepdims=True)) + lam_init)
    l1 = l_scr[0]
    l2 = l_scr[1]
    o = (acc_scr[0] / jnp.concatenate([l1, l1], axis=1)
         - lam * (acc_scr[1] / jnp.concatenate([l2, l2], axis=1)))
    o_ref[...] = (_rms(o, sg_ref[...], SUBLN_EPS) * (1.0 - lam_init)).astype(o_ref.dtype)


def _diff_attention(proj3, bias_tiles, bias_far, lq1, lk1, lq2, lk2, subln_g, lam_init):
    bsz, seq, _ = proj3.shape
    blk = ATTN_BLK
    n_blk = seq // blk
    vec = lambda n: pl.BlockSpec((1, n), lambda b, h, i: (0, 0))
    return pl.pallas_call(
        functools.partial(_attn_kernel, blk=blk, n_blk=n_blk, lam_init=lam_init),
        grid=(bsz, B_HEADS, n_blk),
        in_specs=[pl.BlockSpec((None, blk, B_HEAD_W), lambda b, h, i: (b, i, COL_Q * HEADS_PER_COL + h)),
                  pl.BlockSpec((None, seq, B_HEAD_W), lambda b, h, i: (b, 0, COL_K * HEADS_PER_COL + h)),
                  pl.BlockSpec((None, seq, B_HEAD_W), lambda b, h, i: (b, 0, COL_V * HEADS_PER_COL + h)),
                  pl.BlockSpec((None, 3, blk, blk), lambda b, h, i: (h, 0, 0, 0)),
                  pl.BlockSpec((None, 2, 8, 128), lambda b, h, i: (h, 0, 0, 0)),
                  vec(B_HEAD_DIM), vec(B_HEAD_DIM), vec(B_HEAD_DIM), vec(B_HEAD_DIM), vec(B_HEAD_W)],
        out_specs=pl.BlockSpec((None, blk, B_HEAD_W), lambda b, h, i: (b, i, h)),
        out_shape=jax.ShapeDtypeStruct((bsz, seq, D_MODEL), BF16),
        scratch_shapes=[pltpu.VMEM((2, blk, 128), F32),
                        pltpu.VMEM((2, blk, 128), F32),
                        pltpu.VMEM((2, blk, B_HEAD_W), F32)],
        compiler_params=_params("parallel", "parallel", "arbitrary"),
        name="diff_attention",
    )(proj3, proj3, proj3, bias_tiles, bias_far, lq1, lk1, lq2, lk2, subln_g)


def _merge_kernel(x_ref, oa_ref, ob_ref, ga_ref, gb_ref, wpa_ref, wpb_ref, wo_ref, o_ref):
    pa = _dot(oa_ref[...], wpa_ref[...])
    pb = _dot(ob_ref[...], wpb_ref[...])
    merged = (jax.nn.sigmoid(ga_ref[...].astype(F32)) * pa
              + jax.nn.sigmoid(gb_ref[...].astype(F32)) * pb)
    o_ref[...] = x_ref[...] + _dot(merged.astype(BF16), wo_ref[...])


def _merge(x2d, o_a, o_b, proj, w_pa, w_pb, w_out):
    t = x2d.shape[0]
    tm = min(MERGE_TM, t)
    row = lambda col: pl.BlockSpec((tm, D_MODEL), lambda i: (i, col))
    weight = pl.BlockSpec((D_MODEL, D_MODEL), lambda i: (0, 0), pipeline_mode=pl.Buffered(1))
    return pl.pallas_call(
        _merge_kernel,
        grid=(t // tm,),
        in_specs=[row(0), row(0), row(0), row(COL_GA), row(COL_GB), weight, weight, weight],
        out_specs=row(0),
        out_shape=jax.ShapeDtypeStruct((t, D_MODEL), F32),
        compiler_params=_params("parallel"),
        name="merge_out_proj",
    )(x2d, o_a, o_b, proj, proj, w_pa, w_pb, w_out)


def _mem_kv_kernel(m_ref, g_ref, w_ref, o_ref):
    o_ref[...] = _dot(_rms(m_ref[...], g_ref[...], EPS).astype(BF16), w_ref[...]).astype(o_ref.dtype)


def _mem_kv(mem2d, g, w_kv):
    t = mem2d.shape[0]
    return pl.pallas_call(
        _mem_kv_kernel,
        grid=(t // N_MEM,),
        in_specs=[pl.BlockSpec((N_MEM, D_MODEL), lambda i: (i, 0)),
                  pl.BlockSpec((1, D_MODEL), lambda i: (0, 0)),
                  pl.BlockSpec((D_MODEL, 2 * C_WIDTH), lambda i: (0, 0))],
        out_specs=pl.BlockSpec((N_MEM, 2 * C_WIDTH), lambda i: (i, 0)),
        out_shape=jax.ShapeDtypeStruct((t, 2 * C_WIDTH), BF16),
        compiler_params=_params("parallel"),
        name="mem_kv",
    )(mem2d, g, w_kv)


def _cross_kernel(x_ref, g_ref, wq_ref, kv_ref, wo_ref, o_ref):
    xf = x_ref[...]
    xn = _rms(xf, g_ref[...], EPS).astype(BF16)
    q = (_dot(xn, wq_ref[...]) * (C_HEAD_DIM ** -0.5 * LOG2E)).astype(BF16)
    heads = []
    for h in range(C_HEADS):
        lanes = slice(h * C_HEAD_DIM, (h + 1) * C_HEAD_DIM)
        v_lanes = slice(C_WIDTH + h * C_HEAD_DIM, C_WIDTH + (h + 1) * C_HEAD_DIM)
        s = _dot_nt(q[:, lanes], kv_ref[:, lanes])
        p = jnp.exp2(s - jnp.max(s, axis=1, keepdims=True))
        oh = _dot(p.astype(BF16), kv_ref[:, v_lanes]) / jnp.sum(p, axis=1, keepdims=True)
        heads.append(oh.astype(BF16))
    o_ref[...] = xf + _dot(jnp.concatenate(heads, axis=1), wo_ref[...])


def _cross(x1, g, w_cq, kv, w_co, bsz, seq):
    tm = min(CROSS_TM, seq)
    per_b = seq // tm
    return pl.pallas_call(
        _cross_kernel,
        grid=(bsz, per_b),
        in_specs=[pl.BlockSpec((tm, D_MODEL), lambda b, i: (b * per_b + i, 0)),
                  pl.BlockSpec((1, D_MODEL), lambda b, i: (0, 0)),
                  pl.BlockSpec((D_MODEL, C_WIDTH), lambda b, i: (0, 0)),
                  pl.BlockSpec((N_MEM, 2 * C_WIDTH), lambda b, i: (b, 0)),
                  pl.BlockSpec((C_WIDTH, D_MODEL), lambda b, i: (0, 0))],
        out_specs=pl.BlockSpec((tm, D_MODEL), lambda b, i: (b * per_b + i, 0)),
        out_shape=jax.ShapeDtypeStruct(x1.shape, F32),
        compiler_params=_params("parallel", "parallel"),
        name="cross_attention",
    )(x1, g, w_cq, kv, w_co)


def _ffn_kernel(x_ref, g_ref, wg_ref, wu_ref, wo_ref, gf_ref, o_ref, xn_scr, acc_scr):
    j = pl.program_id(1)

    @pl.when(j == 0)
    def _():
        xn_scr[...] = _rms(x_ref[...], g_ref[...], EPS).astype(BF16)
        acc_scr[...] = jnp.zeros(acc_scr.shape, F32)

    xn = xn_scr[...]
    act = jax.nn.silu(_dot(xn, wg_ref[...])) * _dot(xn, wu_ref[...])
    acc_scr[...] += _dot(act.astype(BF16), wo_ref[...])

    @pl.when(j == pl.num_programs(1) - 1)
    def _():
        o_ref[...] = _rms(x_ref[...] + acc_scr[...], gf_ref[...], EPS)


def _ffn(x2, g, w_in, w_out, g_final):
    t = x2.shape[0]
    tm = min(FFN_TM, t)
    n_fc = D_FF // FFN_FC
    return pl.pallas_call(
        _ffn_kernel,
        grid=(t // tm, n_fc),
        in_specs=[pl.BlockSpec((tm, D_MODEL), lambda i, j: (i, 0)),
                  pl.BlockSpec((1, D_MODEL), lambda i, j: (0, 0)),
                  pl.BlockSpec((D_MODEL, FFN_FC), lambda i, j: (0, j)),
                  pl.BlockSpec((D_MODEL, FFN_FC), lambda i, j: (0, j + n_fc)),
                  pl.BlockSpec((FFN_FC, D_MODEL), lambda i, j: (j, 0)),
                  pl.BlockSpec((1, D_MODEL), lambda i, j: (0, 0))],
        out_specs=pl.BlockSpec((tm, D_MODEL), lambda i, j: (i, 0)),
        out_shape=jax.ShapeDtypeStruct((t, D_MODEL), F32),
        scratch_shapes=[pltpu.VMEM((tm, D_MODEL), BF16), pltpu.VMEM((tm, D_MODEL), F32)],
        compiler_params=_params("parallel", "arbitrary"),
        name="ffn_final_norm",
    )(x2, g, w_in, w_in, w_out, g_final)


def _trunk(x, mem, w):
    bsz, seq, _ = x.shape
    t = bsz * seq
    x2d = x.reshape(t, D_MODEL)
    lam_init = 0.8 - 0.6 * math.exp(-0.3 * 0)

    proj = _in_proj(x2d, w["norm_mix_g"], w["w_in"], w["colscale"])
    o_a = _spatial(proj, w["ln_v_g"], w["ln_v_b"], w["w_spatial"], w["b_spatial_full"])
    o_b = _diff_attention(proj.reshape(bsz, seq, N_IN), w["bias_tiles"], w["bias_far"],
                          w["lambda_q1"], w["lambda_k1"], w["lambda_q2"], w["lambda_k2"],
                          w["subln_g"], lam_init)
    x1 = _merge(x2d, o_a, o_b.reshape(t, D_MODEL), proj, w["w_proj_a"], w["w_proj_b"], w["w_out"])
    kv = _mem_kv(mem.reshape(bsz * N_MEM, D_MODEL), w["norm_mem_g"], w["w_ckv"])
    x2 = _cross(x1, w["norm_cross_g"], w["w_cq"], kv, w["w_co"], bsz, seq)
    y = _ffn(x2, w["norm_ffn_g"], w["w_ffn_in"], w["w_ffn_out"], w["norm_final_g"])
    return y.reshape(bsz, seq, D_MODEL)


def kernel(x_prompt, x_sample, mem_prompt, mem_sample, rel_bias_table, norm_mix_g, w_in, ln_v_g, ln_v_b, w_spatial, b_spatial, lambda_q1, lambda_k1, lambda_q2, lambda_k2, subln_g, w_proj_a, w_proj_b, w_out, norm_cross_g, norm_mem_g, w_cq, w_ck, w_cv, w_co, norm_ffn_g, w_ffn_in, w_ffn_out, norm_final_g):
    assert w_in.shape[0] == 1, "single-layer trunk"
    colscale = jnp.ones((7, D_MODEL), F32).at[COL_Q].set(B_HEAD_DIM ** -0.5 * LOG2E).reshape(1, N_IN)
    b_full = jnp.repeat(jnp.transpose(b_spatial[0]), A_GROUP_W, axis=1)
    bias_tiles, bias_far = _bias_tiles(rel_bias_table, ATTN_BLK)
    w = {
        "colscale": colscale,
        "b_spatial_full": b_full,
        "bias_tiles": bias_tiles,
        "bias_far": bias_far,
        "norm_mix_g": norm_mix_g,
        "w_in": w_in[0].astype(BF16),
        "ln_v_g": ln_v_g,
        "ln_v_b": ln_v_b,
        "w_spatial": w_spatial[0].astype(BF16),
        "lambda_q1": lambda_q1,
        "lambda_k1": lambda_k1,
        "lambda_q2": lambda_q2,
        "lambda_k2": lambda_k2,
        "subln_g": subln_g,
        "w_proj_a": w_proj_a[0].astype(BF16),
        "w_proj_b": w_proj_b[0].astype(BF16),
        "w_out": w_out[0].astype(BF16),
        "norm_cross_g": norm_cross_g,
        "norm_mem_g": norm_mem_g,
        "w_cq": w_cq[0].astype(BF16),
        "w_ckv": jnp.concatenate([w_ck[0], w_cv[0]], axis=1).astype(BF16),
        "w_co": w_co[0].astype(BF16),
        "norm_ffn_g": norm_ffn_g,
        "w_ffn_in": w_ffn_in[0].astype(BF16),
        "w_ffn_out": w_ffn_out[0].astype(BF16),
        "norm_final_g": norm_final_g.reshape(1, D_MODEL),
    }
    return (_trunk(x_prompt, mem_prompt, w), _trunk(x_sample, mem_sample, w))
```

```python
import functools
import math

import jax
import jax.numpy as jnp
from jax import lax
from jax.experimental import pallas as pl
from jax.experimental.pallas import tpu as pltpu

F32 = jnp.float32
BF16 = jnp.bfloat16

D_MODEL = 2048
N_MEM = 256
CHUNK = 128
A_GROUPS = 8
A_GROUP_W = D_MODEL // A_GROUPS
B_HEADS = 8
B_HEAD_DIM = 128
B_HEAD_W = 2 * B_HEAD_DIM
N_BUCKETS = 32
MAX_DISTANCE = 128
C_HEADS = 4
C_HEAD_DIM = 128
C_WIDTH = C_HEADS * C_HEAD_DIM
D_FF = -(-8 * D_MODEL // (3 * 256)) * 256
N_IN = 7 * D_MODEL
EPS = 1e-6
SUBLN_EPS = 1e-5
LOG2E = 1.4426950408889634

COL_U, COL_VA, COL_Q, COL_K, COL_V, COL_GA, COL_GB = range(7)
HEADS_PER_COL = D_MODEL // B_HEAD_W

MXU_TILE = 256
VMEM_LIMIT = 56 * 1024 * 1024
IN_TM, IN_TN = 1024, 1024
SG_TM = 512
ATTN_BLK = 512
MERGE_TM = 256
CROSS_TM = 512
FFN_TM, FFN_FC = 512, 512


def _params(*sem):
    return pltpu.CompilerParams(dimension_semantics=sem, vmem_limit_bytes=VMEM_LIMIT)


def _rms(xf, g, eps):
    return xf * lax.rsqrt(jnp.mean(xf * xf, axis=-1, keepdims=True) + eps) * g


def _dot(a, b):
    return jnp.dot(a, b, preferred_element_type=F32)


def _dot_nt(a, b):
    return lax.dot_general(a, b, (((1,), (1,)), ((), ())), preferred_element_type=F32)


def _bias_kernel(table_ref, tiles_ref, far_ref, *, blk):
    h = pl.program_id(0)
    half = N_BUCKETS // 2
    max_exact = half // 2

    def bias_of(rel):
        n = jnp.abs(rel)
        nf = jnp.maximum(n, 1).astype(F32)
        large = max_exact + (jnp.log(nf / max_exact) / math.log(MAX_DISTANCE / max_exact)
                             * (half - max_exact)).astype(jnp.int32)
        large = jnp.minimum(large, half - 1)
        bucket = jnp.where(rel > 0, half, 0) + jnp.where(n < max_exact, n, large)
        val = jnp.zeros(rel.shape, F32)
        for b in range(N_BUCKETS):
            val = jnp.where(bucket == b, table_ref[b, h], val)
        return val * LOG2E

    row = lax.broadcasted_iota(jnp.int32, (blk, blk), 0)
    col = lax.broadcasted_iota(jnp.int32, (blk, blk), 1)
    for d in (-1, 0, 1):
        tiles_ref[d + 1] = bias_of(d * blk + col - row)
    far = jnp.full((8, 128), MAX_DISTANCE, jnp.int32)
    far_ref[0] = bias_of(-far)
    far_ref[1] = bias_of(far)


def _bias_tiles(table, blk):
    return pl.pallas_call(
        functools.partial(_bias_kernel, blk=blk),
        grid=(B_HEADS,),
        in_specs=[pl.BlockSpec(memory_space=pltpu.SMEM)],
        out_specs=[pl.BlockSpec((None, 3, blk, blk), lambda h: (h, 0, 0, 0)),
                   pl.BlockSpec((None, 2, 8, 128), lambda h: (h, 0, 0, 0))],
        out_shape=[jax.ShapeDtypeStruct((B_HEADS, 3, blk, blk), F32),
                   jax.ShapeDtypeStruct((B_HEADS, 2, 8, 128), F32)],
        compiler_params=_params("arbitrary"),
        name="bias_tiles",
    )(table)


def _in_proj_kernel(x_ref, g_ref, w_ref, cs_ref, o_ref, h_scr):
    @pl.when(pl.program_id(1) == 0)
    def _():
        h_scr[...] = _rms(x_ref[...], g_ref[...], EPS).astype(BF16)

    o_ref[...] = (_dot(h_scr[...], w_ref[...]) * cs_ref[...]).astype(o_ref.dtype)


def _in_proj(x2d, g, w, colscale):
    t = x2d.shape[0]
    tm = min(IN_TM, t)
    return pl.pallas_call(
        _in_proj_kernel,
        grid=(t // tm, N_IN // IN_TN),
        in_specs=[pl.BlockSpec((tm, D_MODEL), lambda i, j: (i, 0)),
                  pl.BlockSpec((1, D_MODEL), lambda i, j: (0, 0)),
                  pl.BlockSpec((D_MODEL, IN_TN), lambda i, j: (0, j)),
                  pl.BlockSpec((1, IN_TN), lambda i, j: (0, j))],
        out_specs=pl.BlockSpec((tm, IN_TN), lambda i, j: (i, j)),
        out_shape=jax.ShapeDtypeStruct((t, N_IN), BF16),
        scratch_shapes=[pltpu.VMEM((tm, D_MODEL), BF16)],
        compiler_params=_params("parallel", "arbitrary"),
        name="in_proj",
    )(x2d, g, w, colscale)


def _spatial_kernel(u_ref, v_ref, lng_ref, lnb_ref, ws_ref, bs_ref, o_ref, *, n_chunks):
    for c in range(n_chunks):
        rows = slice(c * CHUNK, (c + 1) * CHUNK)
        gu = jax.nn.gelu(u_ref[rows, :].astype(F32))
        gv = jax.nn.gelu(v_ref[rows, :].astype(F32))
        vc = gv - jnp.mean(gv, axis=-1, keepdims=True)
        vn = vc * lax.rsqrt(jnp.mean(vc * vc, axis=-1, keepdims=True) + EPS)
        vn = (vn * lng_ref[...] + lnb_ref[...]).astype(BF16)
        for g in range(A_GROUPS):
            cols = slice(g * A_GROUP_W, (g + 1) * A_GROUP_W)
            mixed = _dot(ws_ref[g], vn[:, cols]) + bs_ref[:, cols]
            o_ref[rows, cols] = (gu[:, cols] * mixed).astype(o_ref.dtype)


def _spatial(proj, ln_g, ln_b, w_s, b_full):
    t = proj.shape[0]
    tm = min(SG_TM, t)
    return pl.pallas_call(
        functools.partial(_spatial_kernel, n_chunks=tm // CHUNK),
        grid=(t // tm,),
        in_specs=[pl.BlockSpec((tm, D_MODEL), lambda i: (i, COL_U)),
                  pl.BlockSpec((tm, D_MODEL), lambda i: (i, COL_VA)),
                  pl.BlockSpec((1, D_MODEL), lambda i: (0, 0)),
                  pl.BlockSpec((1, D_MODEL), lambda i: (0, 0)),
                  pl.BlockSpec((A_GROUPS, CHUNK, CHUNK), lambda i: (0, 0, 0)),
                  pl.BlockSpec((CHUNK, D_MODEL), lambda i: (0, 0))],
        out_specs=pl.BlockSpec((tm, D_MODEL), lambda i: (i, 0)),
        out_shape=jax.ShapeDtypeStruct((t, D_MODEL), BF16),
        compiler_params=_params("parallel"),
        name="spatial_gating",
    )(proj, proj, ln_g, ln_b, w_s, b_full)


def _attn_kernel(far_ref, q_ref, k_ref, v_ref, bias_ref, lq1_ref, lk1_ref, lq2_ref, lk2_ref, sg_ref,
                 o_ref, qs_scr, s_scr, rm_scr, m_scr, l_scr, acc_scr, *, blk, n_blk, lam_init):
    h = pl.program_id(1)
    i = pl.program_id(2)
    rows = 2 * blk
    q = q_ref[...]
    first_half = lax.broadcasted_iota(jnp.int32, q.shape, 1) < B_HEAD_DIM
    zero = jnp.zeros_like(q)
    qs_scr[0:blk, :] = jnp.where(first_half, q, zero)
    qs_scr[blk:rows, :] = jnp.where(first_half, zero, q)
    m_scr[...] = jnp.full(m_scr.shape, -jnp.inf, F32)
    l_scr[...] = jnp.zeros(l_scr.shape, F32)
    acc_scr[...] = jnp.zeros(acc_scr.shape, F32)
    n_sub = blk // MXU_TILE

    n_left = jnp.maximum(i - 1, 0)
    right0 = jnp.minimum(i + 2, n_blk)
    n_far = n_left + n_blk - right0

    def step_block(u):
        jn = i + jnp.where(u == 0, 0, jnp.where(u == 1, -1, 1))
        near_c = jnp.where(jnp.logical_and(jn >= 0, jn < n_blk), 0.0, -jnp.inf)
        t = u - 3
        on_right = t >= n_left
        far_j = jnp.where(on_right, t - n_left + right0, t)
        is_near = u < 3
        j = jnp.where(is_near, jnp.clip(jn, 0, n_blk - 1), far_j)
        c = jnp.where(is_near, near_c, far_ref[h, on_right.astype(jnp.int32)])
        return pl.multiple_of(j * blk, blk), c

    def lane_fold(x, op):
        parts = [x[:, k * 128:(k + 1) * 128] for k in range(MXU_TILE // 128)]
        return functools.reduce(op, parts)

    def scores(u, slot, tile_idx=None):
        start, c = step_block(u)
        part_max = None
        for t in range(n_sub):
            cols = slice(t * MXU_TILE, (t + 1) * MXU_TILE)
            s = _dot_nt(qs_scr[...], k_ref[pl.ds(start + t * MXU_TILE, MXU_TILE), :])
            if tile_idx is not None:
                bias = bias_ref[tile_idx, :, cols]
                s = s + jnp.concatenate([bias, bias], axis=0)
            s_scr[slot, :, cols] = s
            folded = lane_fold(s, jnp.maximum)
            part_max = folded if part_max is None else jnp.maximum(part_max, folded)
        rm_scr[slot] = jnp.broadcast_to(jnp.max(part_max, axis=1, keepdims=True) + c, (rows, 128))

    def accumulate(u, slot):
        start, c = step_block(u)
        m_prev = m_scr[...]
        m_next = jnp.maximum(m_prev, rm_scr[slot])
        alpha = jnp.exp2(m_prev - m_next)
        shift = m_next - c
        shift = jnp.concatenate([shift] * (MXU_TILE // 128), axis=1)
        l_new = alpha * l_scr[...]
        pv = None
        for t in range(n_sub):
            cols = slice(t * MXU_TILE, (t + 1) * MXU_TILE)
            p = jnp.exp2(s_scr[slot, :, cols] - shift)
            l_new = l_new + lane_fold(p, jnp.add)
            part = _dot(p.astype(BF16), v_ref[pl.ds(start + t * MXU_TILE, MXU_TILE), :])
            pv = part if pv is None else pv + part
        l_scr[...] = l_new
        m_scr[...] = m_next
        acc_scr[...] = acc_scr[...] * jnp.concatenate([alpha, alpha], axis=1) + pv

    scores(0, 0, tile_idx=1)
    scores(1, 1, tile_idx=0)
    accumulate(0, 0)
    scores(2, 0, tile_idx=2)
    accumulate(1, 1)

    def pair_body(p, carry):
        u = 3 + 2 * p
        scores(u, 1)
        accumulate(u - 1, 0)
        scores(u + 1, 0)
        accumulate(u, 1)
        return carry

    n_pairs = lax.shift_right_logical(n_far, 1)
    lax.fori_loop(0, n_pairs, pair_body, 0)
    u_end = 3 + 2 * n_pairs
    odd = (n_far & 1) == 1

    @pl.when(odd)
    def _():
        scores(u_end, 1)
        accumulate(u_end - 1, 0)
        accumulate(u_end, 1)

    @pl.when(jnp.logical_not(odd))
    def _():
        accumulate(u_end - 1, 0)

    lam = (jnp.exp(jnp.sum(lq1_ref[...] * lk1_ref[...], axis=-1, keepdims=True))
           - jnp.exp(jnp.sum(lq2_ref[...] * lk2_ref[...], axis=-1, keepdims=True)) + lam_init)
    l1 = jnp.sum(l_scr[0:blk, :], axis=1, keepdims=True)
    l2 = jnp.sum(l_scr[blk:rows, :], axis=1, keepdims=True)
    o = acc_scr[0:blk, :] / l1 - lam * (acc_scr[blk:rows, :] / l2)
    o_ref[...] = (_rms(o, sg_ref[...], SUBLN_EPS) * (1.0 - lam_init)).astype(o_ref.dtype)


def _diff_attention(proj3, bias_tiles, bias_far, lq1, lk1, lq2, lk2, subln_g, lam_init):
    bsz, seq, _ = proj3.shape
    blk = ATTN_BLK
    n_blk = seq // blk
    vec = lambda n: pl.BlockSpec((1, n), lambda b, h, i: (0, 0))
    resident = dict(pipeline_mode=pl.Buffered(1))
    return pl.pallas_call(
        functools.partial(_attn_kernel, blk=blk, n_blk=n_blk, lam_init=lam_init),
        grid=(bsz, B_HEADS, n_blk),
        in_specs=[pl.BlockSpec(memory_space=pltpu.SMEM),
                  pl.BlockSpec((None, blk, B_HEAD_W), lambda b, h, i: (b, i, COL_Q * HEADS_PER_COL + h)),
                  pl.BlockSpec((None, seq, B_HEAD_W), lambda b, h, i: (b, 0, COL_K * HEADS_PER_COL + h),
                               **resident),
                  pl.BlockSpec((None, seq, B_HEAD_W), lambda b, h, i: (b, 0, COL_V * HEADS_PER_COL + h),
                               **resident),
                  pl.BlockSpec((None, 3, blk, blk), lambda b, h, i: (h, 0, 0, 0), **resident),
                  vec(B_HEAD_DIM), vec(B_HEAD_DIM), vec(B_HEAD_DIM), vec(B_HEAD_DIM), vec(B_HEAD_W)],
        out_specs=pl.BlockSpec((None, blk, B_HEAD_W), lambda b, h, i: (b, i, h)),
        out_shape=jax.ShapeDtypeStruct((bsz, seq, D_MODEL), BF16),
        scratch_shapes=[pltpu.VMEM((2 * blk, B_HEAD_W), BF16),
                        pltpu.VMEM((2, 2 * blk, blk), F32),
                        pltpu.VMEM((2, 2 * blk, 128), F32),
                        pltpu.VMEM((2 * blk, 128), F32),
                        pltpu.VMEM((2 * blk, 128), F32),
                        pltpu.VMEM((2 * blk, B_HEAD_W), F32)],
        compiler_params=_params("parallel", "parallel", "arbitrary"),
        name="diff_attention",
    )(bias_far, proj3, proj3, proj3, bias_tiles, lq1, lk1, lq2, lk2, subln_g)


def _merge_kernel(x_ref, oa_ref, ob_ref, ga_ref, gb_ref, wpa_ref, wpb_ref, wo_ref, o_ref):
    pa = _dot(oa_ref[...], wpa_ref[...])
    pb = _dot(ob_ref[...], wpb_ref[...])
    merged = (jax.nn.sigmoid(ga_ref[...].astype(F32)) * pa
              + jax.nn.sigmoid(gb_ref[...].astype(F32)) * pb)
    o_ref[...] = x_ref[...] + _dot(merged.astype(BF16), wo_ref[...])


def _merge(x2d, o_a, o_b, proj, w_pa, w_pb, w_out):
    t = x2d.shape[0]
    tm = min(MERGE_TM, t)
    row = lambda col: pl.BlockSpec((tm, D_MODEL), lambda i: (i, col))
    weight = pl.BlockSpec((D_MODEL, D_MODEL), lambda i: (0, 0), pipeline_mode=pl.Buffered(1))
    return pl.pallas_call(
        _merge_kernel,
        grid=(t // tm,),
        in_specs=[row(0), row(0), row(0), row(COL_GA), row(COL_GB), weight, weight, weight],
        out_specs=row(0),
        out_shape=jax.ShapeDtypeStruct((t, D_MODEL), F32),
        compiler_params=_params("parallel"),
        name="merge_out_proj",
    )(x2d, o_a, o_b, proj, proj, w_pa, w_pb, w_out)


def _mem_kv_kernel(m_ref, g_ref, w_ref, o_ref):
    o_ref[...] = _dot(_rms(m_ref[...], g_ref[...], EPS).astype(BF16), w_ref[...]).astype(o_ref.dtype)


def _mem_kv(mem2d, g, w_kv):
    t = mem2d.shape[0]
    return pl.pallas_call(
        _mem_kv_kernel,
        grid=(t // N_MEM,),
        in_specs=[pl.BlockSpec((N_MEM, D_MODEL), lambda i: (i, 0)),
                  pl.BlockSpec((1, D_MODEL), lambda i: (0, 0)),
                  pl.BlockSpec((D_MODEL, 2 * C_WIDTH), lambda i: (0, 0))],
        out_specs=pl.BlockSpec((N_MEM, 2 * C_WIDTH), lambda i: (i, 0)),
        out_shape=jax.ShapeDtypeStruct((t, 2 * C_WIDTH), BF16),
        compiler_params=_params("parallel"),
        name="mem_kv",
    )(mem2d, g, w_kv)


def _cross_kernel(x_ref, g_ref, wq_ref, kv_ref, wo_ref, o_ref):
    xf = x_ref[...]
    xn = _rms(xf, g_ref[...], EPS).astype(BF16)
    q = (_dot(xn, wq_ref[...]) * (C_HEAD_DIM ** -0.5 * LOG2E)).astype(BF16)
    heads = []
    for h in range(C_HEADS):
        lanes = slice(h * C_HEAD_DIM, (h + 1) * C_HEAD_DIM)
        v_lanes = slice(C_WIDTH + h * C_HEAD_DIM, C_WIDTH + (h + 1) * C_HEAD_DIM)
        s = _dot_nt(q[:, lanes], kv_ref[:, lanes])
        p = jnp.exp2(s - jnp.max(s, axis=1, keepdims=True))
        oh = _dot(p.astype(BF16), kv_ref[:, v_lanes]) / jnp.sum(p, axis=1, keepdims=True)
        heads.append(oh.astype(BF16))
    o_ref[...] = xf + _dot(jnp.concatenate(heads, axis=1), wo_ref[...])


def _cross(x1, g, w_cq, kv, w_co, bsz, seq):
    tm = min(CROSS_TM, seq)
    per_b = seq // tm
    return pl.pallas_call(
        _cross_kernel,
        grid=(bsz, per_b),
        in_specs=[pl.BlockSpec((tm, D_MODEL), lambda b, i: (b * per_b + i, 0)),
                  pl.BlockSpec((1, D_MODEL), lambda b, i: (0, 0)),
                  pl.BlockSpec((D_MODEL, C_WIDTH), lambda b, i: (0, 0)),
                  pl.BlockSpec((N_MEM, 2 * C_WIDTH), lambda b, i: (b, 0)),
                  pl.BlockSpec((C_WIDTH, D_MODEL), lambda b, i: (0, 0))],
        out_specs=pl.BlockSpec((tm, D_MODEL), lambda b, i: (b * per_b + i, 0)),
        out_shape=jax.ShapeDtypeStruct(x1.shape, F32),
        compiler_params=_params("parallel", "parallel"),
        name="cross_attention",
    )(x1, g, w_cq, kv, w_co)


def _ffn_kernel(x_ref, g_ref, wg_ref, wu_ref, wo_ref, gf_ref, o_ref, xn_scr, acc_scr):
    j = pl.program_id(1)

    @pl.when(j == 0)
    def _():
        xn_scr[...] = _rms(x_ref[...], g_ref[...], EPS).astype(BF16)
        acc_scr[...] = jnp.zeros(acc_scr.shape, F32)

    xn = xn_scr[...]
    act = jax.nn.silu(_dot(xn, wg_ref[...])) * _dot(xn, wu_ref[...])
    acc_scr[...] += _dot(act.astype(BF16), wo_ref[...])

    @pl.when(j == pl.num_programs(1) - 1)
    def _():
        o_ref[...] = _rms(x_ref[...] + acc_scr[...], gf_ref[...], EPS)


def _ffn(x2, g, w_in, w_out, g_final):
    t = x2.shape[0]
    tm = min(FFN_TM, t)
    n_fc = D_FF // FFN_FC
    return pl.pallas_call(
        _ffn_kernel,
        grid=(t // tm, n_fc),
        in_specs=[pl.BlockSpec((tm, D_MODEL), lambda i, j: (i, 0)),
                  pl.BlockSpec((1, D_MODEL), lambda i, j: (0, 0)),
                  pl.BlockSpec((D_MODEL, FFN_FC), lambda i, j: (0, j)),
                  pl.BlockSpec((D_MODEL, FFN_FC), lambda i, j: (0, j + n_fc)),
                  pl.BlockSpec((FFN_FC, D_MODEL), lambda i, j: (j, 0)),
                  pl.BlockSpec((1, D_MODEL), lambda i, j: (0, 0))],
        out_specs=pl.BlockSpec((tm, D_MODEL), lambda i, j: (i, 0)),
        out_shape=jax.ShapeDtypeStruct((t, D_MODEL), F32),
        scratch_shapes=[pltpu.VMEM((tm, D_MODEL), BF16), pltpu.VMEM((tm, D_MODEL), F32)],
        compiler_params=_params("parallel", "arbitrary"),
        name="ffn_final_norm",
    )(x2, g, w_in, w_in, w_out, g_final)


def _trunk(x, mem, w):
    bsz, seq, _ = x.shape
    t = bsz * seq
    x2d = x.reshape(t, D_MODEL)
    lam_init = 0.8 - 0.6 * math.exp(-0.3 * 0)

    proj = _in_proj(x2d, w["norm_mix_g"], w["w_in"], w["colscale"])
    o_a = _spatial(proj, w["ln_v_g"], w["ln_v_b"], w["w_spatial"], w["b_spatial_full"])
    o_b = _diff_attention(proj.reshape(bsz, seq, N_IN), w["bias_tiles"], w["bias_far"],
                          w["lambda_q1"], w["lambda_k1"], w["lambda_q2"], w["lambda_k2"],
                          w["subln_g"], lam_init)
    x1 = _merge(x2d, o_a, o_b.reshape(t, D_MODEL), proj, w["w_proj_a"], w["w_proj_b"], w["w_out"])
    kv = _mem_kv(mem.reshape(bsz * N_MEM, D_MODEL), w["norm_mem_g"], w["w_ckv"])
    x2 = _cross(x1, w["norm_cross_g"], w["w_cq"], kv, w["w_co"], bsz, seq)
    y = _ffn(x2, w["norm_ffn_g"], w["w_ffn_in"], w["w_ffn_out"], w["norm_final_g"])
    return y.reshape(bsz, seq, D_MODEL)


def kernel(x_prompt, x_sample, mem_prompt, mem_sample, rel_bias_table, norm_mix_g, w_in, ln_v_g, ln_v_b, w_spatial, b_spatial, lambda_q1, lambda_k1, lambda_q2, lambda_k2, subln_g, w_proj_a, w_proj_b, w_out, norm_cross_g, norm_mem_g, w_cq, w_ck, w_cv, w_co, norm_ffn_g, w_ffn_in, w_ffn_out, norm_final_g):
    assert w_in.shape[0] == 1, "single-layer trunk"
    colscale = jnp.ones((7, D_MODEL), F32).at[COL_Q].set(B_HEAD_DIM ** -0.5 * LOG2E).reshape(1, N_IN)
    b_full = jnp.repeat(jnp.transpose(b_spatial[0]), A_GROUP_W, axis=1)
    bias_tiles, bias_far = _bias_tiles(rel_bias_table, ATTN_BLK)
    w = {
        "colscale": colscale,
        "b_spatial_full": b_full,
        "bias_tiles": bias_tiles,
        "bias_far": bias_far[:, :, 0, 0],
        "norm_mix_g": norm_mix_g,
        "w_in": w_in[0].astype(BF16),
        "ln_v_g": ln_v_g,
        "ln_v_b": ln_v_b,
        "w_spatial": w_spatial[0].astype(BF16),
        "lambda_q1": lambda_q1,
        "lambda_k1": lambda_k1,
        "lambda_q2": lambda_q2,
        "lambda_k2": lambda_k2,
        "subln_g": subln_g,
        "w_proj_a": w_proj_a[0].astype(BF16),
        "w_proj_b": w_proj_b[0].astype(BF16),
        "w_out": w_out[0].astype(BF16),
        "norm_cross_g": norm_cross_g,
        "norm_mem_g": norm_mem_g,
        "w_cq": w_cq[0].astype(BF16),
        "w_ckv": jnp.concatenate([w_ck[0], w_cv[0]], axis=1).astype(BF16),
        "w_co": w_co[0].astype(BF16),
        "norm_ffn_g": norm_ffn_g,
        "w_ffn_in": w_ffn_in[0].astype(BF16),
        "w_ffn_out": w_ffn_out[0].astype(BF16),
        "norm_final_g": norm_final_g.reshape(1, D_MODEL),
    }
    return (_trunk(x_prompt, mem_prompt, w), _trunk(x_sample, mem_sample, w))
```
